```python
import jax
import jax.numpy as jnp
from jax import lax
import numpy as np

D_MODEL = 2048
BATCH = 8
SEQ = 2048
DEPTH = 2
DEC_BATCH = 32
DEC_SEQ = 1
PAST_LEN = 8192
PAGE_SIZE = 128

GLA_HEADS = 4
GLA_DK = D_MODEL // 2
GLA_DV = D_MODEL
GLA_HEAD_K = GLA_DK // GLA_HEADS
GLA_HEAD_V = GLA_DV // GLA_HEADS
GLA_GATE_RANK = 16
GLA_GATE_NORM = 16.0
GLA_CHUNK = 32
GLA_IN = 2 * GLA_DK + 2 * GLA_DV + GLA_GATE_RANK
DIL_GROUPS = ((128, 1), (512, 4), (2048, 16))
DIL_HEADS = 8
DIL_HEAD_DIM = 128
DIL_WIDTH = DIL_HEADS * DIL_HEAD_DIM
DIL_BLOCK = 128
ROPE_THETA = 500000.0
ROPE_DIM = DIL_HEAD_DIM // 4
D_FF = 7168
N_EXPERTS = 8
TOP_K = 2
MOE_BLOCK = 128
LN_EPS = 1e-5
RMS_EPS = 1e-6
DEEPNORM_ALPHA = (2 * DEPTH) ** 0.25
DEEPNORM_BETA = (8 * DEPTH) ** -0.25

kernel_name = "gla_dilated_swa_moe_deepnorm_step"


def layer_norm(x, g, b):
    xf = x.astype(jnp.float32)
    mu = jnp.mean(xf, axis=-1, keepdims=True)
    var = jnp.mean(jnp.square(xf - mu), axis=-1, keepdims=True)
    return ((xf - mu) * lax.rsqrt(var + LN_EPS) * g + b).astype(x.dtype)


def post_norm(x, sub, g, b):
    return layer_norm(DEEPNORM_ALPHA * x + sub, g, b)


def gla_recurrence(q, k, v, log_a, s0):
    N, T, H, DK = q.shape
    DV = v.shape[-1]
    C = min(GLA_CHUNK, T)
    nc = -(-T // C)
    pad = nc * C - T

    def prep(a):
        a = jnp.pad(a.astype(jnp.float32), ((0, 0), (0, pad), (0, 0), (0, 0)))
        return a.reshape(N, nc, C, H, a.shape[-1]).transpose(1, 0, 3, 2, 4)

    qc, kc, vc, gc = prep(q), prep(k), prep(v), prep(log_a)
    causal = jnp.tril(jnp.ones((C, C), dtype=bool))

    def step(S, inp):
        qi, ki, vi, gi = inp
        b = jnp.cumsum(gi, axis=2)
        b_last = b[:, :, -1:, :]
        q_e = qi * jnp.exp(b)
        k_e = ki * jnp.exp(-b)
        A = jnp.where(causal, jnp.einsum('nhik,nhjk->nhij', q_e, k_e), 0.0)
        o = jnp.einsum('nhik,nhkv->nhiv', q_e, S) + jnp.einsum('nhij,nhjv->nhiv', A, vi)
        k_dec = ki * jnp.exp(b_last - b)
        S = jnp.exp(b_last[:, :, 0, :])[..., None] * S + jnp.einsum('nhjk,nhjv->nhkv', k_dec, vi)
        return S, o

    s_fin, o = lax.scan(step, s0.astype(jnp.float32), (qc, kc, vc, gc))
    o = o.transpose(1, 0, 3, 2, 4).reshape(N, nc * C, H, DV)[:, :T]
    return o, s_fin


def gla_mixer(x, s0, w_in, w_gate_up, b_gate, norm_g, w_out):
    N, T, _ = x.shape
    proj = x @ w_in
    q, k, v, r, g_low = jnp.split(
        proj, [GLA_DK, 2 * GLA_DK, 2 * GLA_DK + GLA_DV, 2 * GLA_DK + 2 * GLA_DV], axis=-1)
    gate_logit = (g_low @ w_gate_up + b_gate).astype(jnp.float32)
    log_a = jax.nn.log_sigmoid(gate_logit) / GLA_GATE_NORM
    q = q.reshape(N, T, GLA_HEADS, GLA_HEAD_K) * (GLA_HEAD_K ** -0.5)
    k = k.reshape(N, T, GLA_HEADS, GLA_HEAD_K)
    v = v.reshape(N, T, GLA_HEADS, GLA_HEAD_V)
    log_a = log_a.reshape(N, T, GLA_HEADS, GLA_HEAD_K)
    o, s_new = gla_recurrence(q, k, v, log_a, s0)
    o = o * lax.rsqrt(jnp.mean(jnp.square(o), axis=-1, keepdims=True) + RMS_EPS) * norm_g
    o = o.reshape(N, T, GLA_DV).astype(x.dtype) * jax.nn.silu(r)
    return o @ w_out, s_new.astype(s0.dtype)


def rope_partial(x, pos):
    half = ROPE_DIM // 2
    inv = jnp.power(ROPE_THETA, -jnp.arange(half, dtype=jnp.float32) / half)
    ang = pos.astype(jnp.float32)[:, None] * inv[None, :]
    cos = jnp.cos(ang)[None, :, None, :]
    sin = jnp.sin(ang)[None, :, None, :]
    xf = x.astype(jnp.float32)
    x1, x2 = xf[..., :half], xf[..., half:ROPE_DIM]
    out = jnp.concatenate([x1 * cos - x2 * sin, x2 * cos + x1 * sin, xf[..., ROPE_DIM:]], axis=-1)
    return out.astype(x.dtype)


def dil_qkv(x, pos, w_in):
    N, T, _ = x.shape
    proj = (x @ w_in).reshape(N, T, len(DIL_GROUPS), 3, DIL_HEADS, DIL_HEAD_DIM)
    qs = [rope_partial(proj[:, :, g, 0], pos) for g in range(len(DIL_GROUPS))]
    ks = [rope_partial(proj[:, :, g, 1], pos) for g in range(len(DIL_GROUPS))]
    vs = [proj[:, :, g, 2] for g in range(len(DIL_GROUPS))]
    return qs, ks, vs


def band_attention(q, k, v, win_units, block):
    N, L, H, Dh = q.shape
    nb = L // block
    qb = q.reshape(N, nb, block, H, Dh)

    def with_prev(a):
        ab = a.reshape(N, nb, block, H, Dh)
        prev = jnp.concatenate([jnp.zeros_like(ab[:, :1]), ab[:, :-1]], axis=1)
        return jnp.concatenate([prev, ab], axis=2)

    kk, vv = with_prev(k), with_prev(v)
    s = jnp.einsum('nbqhd,nbkhd->nbhqk', qb, kk).astype(jnp.float32) * (DIL_HEAD_DIM ** -0.5)
    qi = jnp.arange(block)[:, None]
    kj = jnp.arange(2 * block)[None, :]
    dist = block + qi - kj
    blk = jnp.arange(nb)[:, None, None]
    valid = (dist >= 0) & (dist <= win_units) & ((blk > 0) | (kj >= block))
    s = jnp.where(valid[None, :, None], s, -jnp.inf)
    m = jnp.max(s, axis=-1, keepdims=True)
    p = jnp.exp(s - m)
    den = jnp.sum(p, axis=-1, keepdims=True)
    o = jnp.einsum('nbhqk,nbkhd->nbqhd', p / den, vv.astype(jnp.float32)).reshape(N, L, H, Dh)
    lse = (m + jnp.log(den))[..., 0].transpose(0, 1, 3, 2).reshape(N, L, H)
    return o, lse


def dilated_group_prompt(q, k, v, window, dil):
    N, S, H, Dh = q.shape
    S_pad = -(-S // dil) * dil
    L = S_pad // dil
    block = min(DIL_BLOCK, L)
    L_pad = -(-L // block) * block

    def to_sub(a):
        a = jnp.pad(a, ((0, 0), (0, S_pad - S), (0, 0), (0, 0)))
        a = a.reshape(N, L, dil, H, Dh).transpose(0, 2, 1, 3, 4).reshape(N * dil, L, H, Dh)
        return jnp.pad(a, ((0, 0), (0, L_pad - L), (0, 0), (0, 0)))

    o, lse = band_attention(to_sub(q), to_sub(k), to_sub(v), window // dil, block)
    o = o[:, :L].reshape(N, dil, L, H, Dh).transpose(0, 2, 1, 3, 4).reshape(N, S_pad, H, Dh)[:, :S]
    lse = lse[:, :L].reshape(N, dil, L, H).transpose(0, 2, 1, 3).reshape(N, S_pad, H)[:, :S]
    return o, lse


def dilated_group_sample(q, k, v, buf, window, dil):
    N, T, H, Dh = q.shape
    Wb = buf.shape[1]
    kv_all = jnp.concatenate([buf, jnp.stack([k, v], axis=2).astype(buf.dtype)], axis=1)
    n_keys = window // dil + 1
    c = Wb + jnp.arange(T)[:, None] - dil * jnp.arange(n_keys)[None, :]
    valid = c >= 0
    kv_g = kv_all[:, jnp.clip(c, 0)]
    s = jnp.einsum('nthd,ntjhd->nthj', q, kv_g[:, :, :, 0].astype(q.dtype)).astype(jnp.float32) * (DIL_HEAD_DIM ** -0.5)
    s = jnp.where(valid[None, :, None, :], s, -jnp.inf)
    m = jnp.max(s, axis=-1, keepdims=True)
    p = jnp.exp(s - m)
    den = jnp.sum(p, axis=-1, keepdims=True)
    o = jnp.einsum('nthj,ntjhd->nthd', p / den, kv_g[:, :, :, 1].astype(jnp.float32))
    lse = (m + jnp.log(den))[..., 0]
    return o, lse, kv_all[:, T:]


def merge_groups(outs, lses, dtype):
    w = jax.nn.softmax(jnp.stack(lses, axis=0), axis=0)
    return jnp.sum(w[..., None] * jnp.stack(outs, axis=0), axis=0).astype(dtype)


def dilated_mixer_prompt(x, pos, w_in, w_out):
    N, T, _ = x.shape
    qs, ks, vs = dil_qkv(x, pos, w_in)
    outs, lses, tails = [], [], []
    for g, (window, dil) in enumerate(DIL_GROUPS):
        o, lse = dilated_group_prompt(qs[g], ks[g], vs[g], window, dil)
        outs.append(o)
        lses.append(lse)
        tails.append(jnp.stack([ks[g], vs[g]], axis=2)[:, T - min(window, T):])
    o = merge_groups(outs, lses, x.dtype).reshape(N, T, DIL_WIDTH)
    return o @ w_out, tails


def dilated_mixer_sample(x, pos, bufs, w_in, w_out):
    N, T, _ = x.shape
    qs, ks, vs = dil_qkv(x, pos, w_in)
    outs, lses, new_bufs = [], [], []
    for g, (window, dil) in enumerate(DIL_GROUPS):
        o, lse, nb = dilated_group_sample(qs[g], ks[g], vs[g], bufs[g], window, dil)
        outs.append(o)
        lses.append(lse)
        new_bufs.append(nb)
    o = merge_groups(outs, lses, x.dtype).reshape(N, T, DIL_WIDTH)
    return o @ w_out, new_bufs


def swiglu(x, w_gate, w_up, w_down):
    return (jax.nn.silu(x @ w_gate) * (x @ w_up)) @ w_down


def moe_swiglu(x, w_router, w_gate, w_up, w_down):
    B, T, D = x.shape
    xt = x.reshape(-1, D)
    n = xt.shape[0]
    logits = (xt @ w_router).astype(jnp.float32)
    top_val, top_idx = lax.top_k(logits, TOP_K)
    gates = jax.nn.softmax(top_val, axis=-1)
    flat_e = top_idx.reshape(-1)
    flat_tok = jnp.repeat(jnp.arange(n, dtype=jnp.int32), TOP_K)
    flat_g = gates.reshape(-1)
    order = jnp.argsort(flat_e)
    e_sorted = flat_e[order]
    counts = jnp.bincount(flat_e, length=N_EXPERTS)
    padded = (counts + MOE_BLOCK - 1) // MOE_BLOCK * MOE_BLOCK
    start = jnp.cumsum(counts) - counts
    pstart = jnp.cumsum(padded) - padded
    dest = pstart[e_sorted] + jnp.arange(n * TOP_K) - start[e_sorted]
    n_blocks = -(-(n * TOP_K) // MOE_BLOCK) + N_EXPERTS
    n_rows = n_blocks * MOE_BLOCK
    row_tok = jnp.full((n_rows,), n, dtype=jnp.int32).at[dest].set(flat_tok[order])
    row_gate = jnp.zeros((n_rows,), jnp.float32).at[dest].set(flat_g[order])
    block_expert = jnp.clip(
        jnp.searchsorted(jnp.cumsum(padded), jnp.arange(n_blocks) * MOE_BLOCK, side='right'), 0, N_EXPERTS - 1)
    x_rows = jnp.concatenate([xt, jnp.zeros((1, D), xt.dtype)], axis=0)[row_tok].reshape(n_blocks, MOE_BLOCK, D)

    def expert_block(args):
        xb, e = args
        return (jax.nn.silu(xb @ w_gate[e]) * (xb @ w_up[e])) @ w_down[e]

    y_rows = lax.map(expert_block, (x_rows, block_expert)).reshape(n_rows, D)
    y = jnp.zeros((n + 1, D), jnp.float32).at[row_tok].add(y_rows.astype(jnp.float32) * row_gate[:, None])[:n]
    return y.astype(x.dtype).reshape(B, T, D)


def setup_inputs(seed: int = 0) -> dict:
    key = jax.random.key(seed)
    ks = jax.random.split(key, 28)

    def nrm(k, shape, scale=1.0):
        return jax.random.normal(k, shape, jnp.float32) * scale

    G = len(DIL_GROUPS)
    cache_shape = lambda w: (DEC_BATCH, min(w, PAST_LEN), 2, DIL_HEADS, DIL_HEAD_DIM)
    return {
        "x_prompt": nrm(ks[0], (BATCH, SEQ, D_MODEL)),
        "x_sample": nrm(ks[1], (DEC_BATCH, DEC_SEQ, D_MODEL)),
        "state_gla": nrm(ks[2], (DEC_BATCH, GLA_HEADS, GLA_HEAD_K, GLA_HEAD_V), 0.1),
        "cache_win128": nrm(ks[3], cache_shape(DIL_GROUPS[0][0])),
        "cache_win512": nrm(ks[4], cache_shape(DIL_GROUPS[1][0])),
        "cache_win2048": nrm(ks[5], cache_shape(DIL_GROUPS[2][0])),
        "gla_w_in": nrm(ks[6], (D_MODEL, GLA_IN), D_MODEL ** -0.5),
        "gla_w_gate_up": nrm(ks[7], (GLA_GATE_RANK, GLA_DK), GLA_GATE_RANK ** -0.5),
        "gla_b_gate": nrm(ks[8], (GLA_DK,), 0.1),
        "gla_norm_g": 1.0 + nrm(ks[9], (GLA_HEAD_V,), 0.02),
        "gla_w_out": nrm(ks[10], (GLA_DV, D_MODEL), GLA_DV ** -0.5 * DEEPNORM_BETA),
        "ln_mix0_g": 1.0 + nrm(ks[11], (D_MODEL,), 0.02),
        "ln_mix0_b": nrm(ks[12], (D_MODEL,), 0.02),
        "ffn_w_gate": nrm(ks[13], (D_MODEL, D_FF), D_MODEL ** -0.5),
        "ffn_w_up": nrm(ks[14], (D_MODEL, D_FF), D_MODEL ** -0.5),
        "ffn_w_down": nrm(ks[15], (D_FF, D_MODEL), D_FF ** -0.5 * DEEPNORM_BETA),
        "ln_ffn0_g": 1.0 + nrm(ks[16], (D_MODEL,), 0.02),
        "ln_ffn0_b": nrm(ks[17], (D_MODEL,), 0.02),
        "dil_w_in": nrm(ks[18], (D_MODEL, G * 3 * DIL_WIDTH), D_MODEL ** -0.5),
        "dil_w_out": nrm(ks[19], (DIL_WIDTH, D_MODEL), DIL_WIDTH ** -0.5 * DEEPNORM_BETA),
        "ln_mix1_g": 1.0 + nrm(ks[20], (D_MODEL,), 0.02),
        "ln_mix1_b": nrm(ks[21], (D_MODEL,), 0.02),
        "moe_w_router": nrm(ks[22], (D_MODEL, N_EXPERTS), D_MODEL ** -0.5),
        "moe_w_gate": nrm(ks[23], (N_EXPERTS, D_MODEL, D_FF), D_MODEL ** -0.5),
        "moe_w_up": nrm(ks[24], (N_EXPERTS, D_MODEL, D_FF), D_MODEL ** -0.5),
        "moe_w_down": nrm(ks[25], (N_EXPERTS, D_FF, D_MODEL), D_FF ** -0.5 * DEEPNORM_BETA),
        "ln_ffn1_g": 1.0 + nrm(ks[26], (D_MODEL,), 0.02),
        "ln_ffn1_b": nrm(ks[27], (D_MODEL,), 0.02),
    }


def reference(x_prompt, x_sample, state_gla, cache_win128, cache_win512, cache_win2048,
              gla_w_in, gla_w_gate_up, gla_b_gate, gla_norm_g, gla_w_out,
              ln_mix0_g, ln_mix0_b, ffn_w_gate, ffn_w_up, ffn_w_down, ln_ffn0_g, ln_ffn0_b,
              dil_w_in, dil_w_out, ln_mix1_g, ln_mix1_b,
              moe_w_router, moe_w_gate, moe_w_up, moe_w_down, ln_ffn1_g, ln_ffn1_b):
    pos_p = jnp.arange(x_prompt.shape[1], dtype=jnp.int32)
    pos_s = PAST_LEN + jnp.arange(x_sample.shape[1], dtype=jnp.int32)
    xp, xs = x_prompt, x_sample
    for layer in range(DEPTH):
        if layer % 2 == 0:
            s0 = jnp.zeros((xp.shape[0], GLA_HEADS, GLA_HEAD_K, GLA_HEAD_V), state_gla.dtype)
            mp, gla_state_p = gla_mixer(xp, s0, gla_w_in, gla_w_gate_up, gla_b_gate, gla_norm_g, gla_w_out)
            ms, gla_state_s = gla_mixer(xs, state_gla, gla_w_in, gla_w_gate_up, gla_b_gate, gla_norm_g, gla_w_out)
            xp = post_norm(xp, mp, ln_mix0_g, ln_mix0_b)
            xs = post_norm(xs, ms, ln_mix0_g, ln_mix0_b)
            xp = post_norm(xp, swiglu(xp, ffn_w_gate, ffn_w_up, ffn_w_down), ln_ffn0_g, ln_ffn0_b)
            xs = post_norm(xs, swiglu(xs, ffn_w_gate, ffn_w_up, ffn_w_down), ln_ffn0_g, ln_ffn0_b)
        else:
            mp, win_p = dilated_mixer_prompt(xp, pos_p, dil_w_in, dil_w_out)
            ms, win_s = dilated_mixer_sample(xs, pos_s, (cache_win128, cache_win512, cache_win2048), dil_w_in, dil_w_out)
            xp = post_norm(xp, mp, ln_mix1_g, ln_mix1_b)
            xs = post_norm(xs, ms, ln_mix1_g, ln_mix1_b)
            xp = post_norm(xp, moe_swiglu(xp, moe_w_router, moe_w_gate, moe_w_up, moe_w_down), ln_ffn1_g, ln_ffn1_b)
            xs = post_norm(xs, moe_swiglu(xs, moe_w_router, moe_w_gate, moe_w_up, moe_w_down), ln_ffn1_g, ln_ffn1_b)
    return (xp, xs, gla_state_p, win_p[0], win_p[1], win_p[2], gla_state_s, win_s[0], win_s[1], win_s[2])
```

```python
import functools

import jax
import jax.numpy as jnp
from jax import lax
from jax.experimental import pallas as pl
from jax.experimental.pallas import tpu as pltpu

F32 = jnp.float32
BF16 = jnp.bfloat16

GLA_HEADS = 4
GLA_GATE_RANK = 16
GLA_GATE_NORM = 16.0
GLA_CHUNK = 32
DIL_GROUPS = ((128, 1), (512, 4), (2048, 16))
DIL_HEADS = 8
DIL_HEAD_DIM = 128
DIL_BLOCK = 128
ROPE_THETA = 500000.0
ROPE_DIM = DIL_HEAD_DIM // 4
N_EXPERTS = 8
TOP_K = 2
LN_EPS = 1e-5
RMS_EPS = 1e-6
DEPTH = 2
DEEPNORM_ALPHA = (2 * DEPTH) ** 0.25
PAST_LEN = 8192

LANES = 128
ROW_TILE = 512
VMEM_LIMIT = 56 * 1024 * 1024


def _cparams(sem):
    return pltpu.CompilerParams(dimension_semantics=sem, vmem_limit_bytes=VMEM_LIMIT)


def _layer_norm(z, g, b):
    mu = jnp.mean(z, axis=-1, keepdims=True)
    zc = z - mu
    var = jnp.mean(zc * zc, axis=-1, keepdims=True)
    return zc * lax.rsqrt(var + LN_EPS) * g + b


def _silu(x):
    return x * (1.0 / (1.0 + jnp.exp(-x)))


def _mm_kernel(x_ref, w_ref, o_ref):
    o_ref[...] = jnp.dot(x_ref[...], w_ref[...], preferred_element_type=F32)


def _matmul(x, w, tn):
    m, k = x.shape
    n = w.shape[1]
    return pl.pallas_call(
        _mm_kernel,
        grid=(m // ROW_TILE, n // tn),
        in_specs=[pl.BlockSpec((ROW_TILE, k), lambda i, j: (i, 0)),
                  pl.BlockSpec((k, tn), lambda i, j: (0, j))],
        out_specs=pl.BlockSpec((ROW_TILE, tn), lambda i, j: (i, j)),
        out_shape=jax.ShapeDtypeStruct((m, n), F32),
        compiler_params=_cparams(("parallel", "arbitrary")),
        name="gla_in_proj",
    )(x, w)


def _gate_kernel(x_ref, wl_ref, wu_ref, b_ref, o_ref):
    g_low = jnp.dot(x_ref[...], wl_ref[...], preferred_element_type=F32)
    z = jnp.dot(g_low.astype(BF16), wu_ref[...], preferred_element_type=F32) + b_ref[...]
    log_sig = jnp.minimum(z, 0.0) - jnp.log1p(jnp.exp(-jnp.abs(z)))
    o_ref[...] = log_sig * (1.0 / GLA_GATE_NORM)


def _gla_gate(xb, w_low, w_up, b_gate):
    m, d = xb.shape
    dk = w_up.shape[1]
    return pl.pallas_call(
        _gate_kernel,
        grid=(m // ROW_TILE,),
        in_specs=[pl.BlockSpec((ROW_TILE, d), lambda i: (i, 0)),
                  pl.BlockSpec((d, LANES), lambda i: (0, 0)),
                  pl.BlockSpec((LANES, dk), lambda i: (0, 0)),
                  pl.BlockSpec((1, dk), lambda i: (0, 0))],
        out_specs=pl.BlockSpec((ROW_TILE, dk), lambda i: (i, 0)),
        out_shape=jax.ShapeDtypeStruct((m, dk), F32),
        compiler_params=_cparams(("parallel",)),
        name="gla_gate",
    )(xb, w_low, w_up, b_gate)


def _split3(x):
    x1 = x.astype(BF16)
    r1 = x - x1.astype(F32)
    x2 = r1.astype(BF16)
    x3 = (r1 - x2.astype(F32)).astype(BF16)
    return x1, x2, x3


def _gla_rec_kernel(q_ref, k_ref, v_ref, r_ref, ga_ref, s0_ref, ng_ref,
                    og_ref, sout_ref, st_ref, *, n_chunks, head_k):
    t = pl.program_id(2)
    c_len = GLA_CHUNK

    @pl.when(t == 0)
    def _():
        st_ref[...] = s0_ref[...].T

    row = lax.broadcasted_iota(jnp.int32, (c_len, c_len), 0)
    col = lax.broadcasted_iota(jnp.int32, (c_len, c_len), 1)
    causal = row >= col
    tril = jnp.where(causal, 1.0, 0.0).astype(BF16)
    scale = head_k ** -0.5
    nt = (((1,), (1,)), ((), ()))
    tn = (((0,), (0,)), ((), ()))

    def chunk(c, carry):
        rows = pl.ds(pl.multiple_of(c * c_len, c_len), c_len)
        g1, g2, g3 = _split3(ga_ref[rows, :])
        b = (jnp.dot(tril, g1, preferred_element_type=F32)
             + jnp.dot(tril, g2, preferred_element_type=F32)
             + jnp.dot(tril, g3, preferred_element_type=F32))
        b_last = b[c_len - 1:c_len, :]
        q = q_ref[rows, :] * scale
        k = k_ref[rows, :]
        v16 = v_ref[rows, :].astype(BF16)
        q_e = (q * jnp.exp(b)).astype(BF16)
        k_e = (k * jnp.exp(-b)).astype(BF16)
        k_dec = (k * jnp.exp(b_last - b)).astype(BF16)
        st = st_ref[...]
        a = lax.dot_general(q_e, k_e, nt, preferred_element_type=F32)
        a = jnp.where(causal, a, 0.0)
        o = (lax.dot_general(q_e, st.astype(BF16), nt, preferred_element_type=F32)
             + jnp.dot(a.astype(BF16), v16, preferred_element_type=F32))
        st_ref[...] = st * jnp.exp(b_last) + lax.dot_general(v16, k_dec, tn, preferred_element_type=F32)
        o = o * lax.rsqrt(jnp.mean(o * o, axis=-1, keepdims=True) + RMS_EPS) * ng_ref[...]
        og_ref[rows, :] = o * _silu(r_ref[rows, :])
        return carry

    lax.fori_loop(0, n_chunks, chunk, 0)

    @pl.when(t == pl.num_programs(2) - 1)
    def _():
        sout_ref[...] = st_ref[...].T


def _gla_recurrence(proj, log_a, s0, norm_g, n_seq, seq_len, t_blk):
    n_heads = GLA_HEADS
    dk = log_a.shape[1] // n_heads
    dv = s0.shape[3]
    kq_blocks = log_a.shape[1] // dk
    kv_off = 2 * log_a.shape[1] // dv
    nt = seq_len // t_blk
    rows = n_seq * seq_len
    row_map = lambda col: (lambda n, h, t: (n * nt + t, col(h)))
    kern = functools.partial(_gla_rec_kernel, n_chunks=t_blk // GLA_CHUNK, head_k=dk)
    return pl.pallas_call(
        kern,
        grid=(n_seq, n_heads, nt),
        in_specs=[pl.BlockSpec((t_blk, dk), row_map(lambda h: h)),
                  pl.BlockSpec((t_blk, dk), row_map(lambda h: kq_blocks + h)),
                  pl.BlockSpec((t_blk, dv), row_map(lambda h: kv_off + h)),
                  pl.BlockSpec((t_blk, dv), row_map(lambda h: kv_off + n_heads + h)),
                  pl.BlockSpec((t_blk, dk), row_map(lambda h: h)),
                  pl.BlockSpec((None, None, dk, dv), lambda n, h, t: (n, h, 0, 0)),
                  pl.BlockSpec((1, dv), lambda n, h, t: (0, 0))],
        out_specs=[pl.BlockSpec((t_blk, dv), row_map(lambda h: h)),
                   pl.BlockSpec((None, None, dk, dv), lambda n, h, t: (n, h, 0, 0))],
        out_shape=[jax.ShapeDtypeStruct((rows, n_heads * dv), F32),
                   jax.ShapeDtypeStruct(s0.shape, F32)],
        scratch_shapes=[pltpu.VMEM((dv, dk), F32)],
        compiler_params=_cparams(("parallel", "parallel", "arbitrary")),
        name="gla_recurrence",
    )(proj, proj, proj, proj, log_a, s0, norm_g)


def _proj_ln_kernel(a_ref, w_ref, res_ref, g_ref, b_ref, of_ref, ob_ref):
    y = jnp.dot(a_ref[...].astype(BF16), w_ref[...], preferred_element_type=F32)
    out = _layer_norm(DEEPNORM_ALPHA * res_ref[...] + y, g_ref[...], b_ref[...])
    of_ref[...] = out
    ob_ref[...] = out.astype(BF16)


def _proj_ln(a, w, res, g, b, name):
    m, k = a.shape
    d = w.shape[1]
    tm = ROW_TILE // 2
    return pl.pallas_call(
        _proj_ln_kernel,
        grid=(m // tm,),
        in_specs=[pl.BlockSpec((tm, k), lambda i: (i, 0)),
                  pl.BlockSpec((k, d), lambda i: (0, 0)),
                  pl.BlockSpec((tm, d), lambda i: (i, 0)),
                  pl.BlockSpec((1, d), lambda i: (0, 0)),
                  pl.BlockSpec((1, d), lambda i: (0, 0))],
        out_specs=[pl.BlockSpec((tm, d), lambda i: (i, 0)),
                   pl.BlockSpec((tm, d), lambda i: (i, 0))],
        out_shape=[jax.ShapeDtypeStruct((m, d), F32), jax.ShapeDtypeStruct((m, d), BF16)],
        compiler_params=_cparams(("parallel",)),
        name=name,
    )(a, w, res, g, b)


def _ffn_kernel(xb_ref, res_ref, wg_ref, wu_ref, wd_ref, g_ref, b_ref, of_ref, ob_ref, acc_ref):
    f = pl.program_id(1)

    @pl.when(f == 0)
    def _():
        acc_ref[...] = jnp.zeros_like(acc_ref)

    x = xb_ref[...]
    gate = jnp.dot(x, wg_ref[...], preferred_element_type=F32)
    up = jnp.dot(x, wu_ref[...], preferred_element_type=F32)
    h = (_silu(gate) * up).astype(BF16)
    acc_ref[...] += jnp.dot(h, wd_ref[...], preferred_element_type=F32)

    @pl.when(f == pl.num_programs(1) - 1)
    def _():
        out = _layer_norm(DEEPNORM_ALPHA * res_ref[...] + acc_ref[...], g_ref[...], b_ref[...])
        of_ref[...] = out
        ob_ref[...] = out.astype(BF16)


def _ffn_ln(xb, res, wg, wu, wd, g, b, tf):
    m, d = xb.shape
    d_ff = wg.shape[1]
    tm = ROW_TILE
    return pl.pallas_call(
        _ffn_kernel,
        grid=(m // tm, d_ff // tf),
        in_specs=[pl.BlockSpec((tm, d), lambda i, f: (i, 0)),
                  pl.BlockSpec((tm, d), lambda i, f: (i, 0)),
                  pl.BlockSpec((d, tf), lambda i, f: (0, f)),
                  pl.BlockSpec((d, tf), lambda i, f: (0, f)),
                  pl.BlockSpec((tf, d), lambda i, f: (f, 0)),
                  pl.BlockSpec((1, d), lambda i, f: (0, 0)),
                  pl.BlockSpec((1, d), lambda i, f: (0, 0))],
        out_specs=[pl.BlockSpec((tm, d), lambda i, f: (i, 0)),
                   pl.BlockSpec((tm, d), lambda i, f: (i, 0))],
        out_shape=[jax.ShapeDtypeStruct((m, d), F32), jax.ShapeDtypeStruct((m, d), BF16)],
        scratch_shapes=[pltpu.VMEM((tm, d), F32)],
        compiler_params=_cparams(("parallel", "arbitrary")),
        name="ffn_swiglu_ln",
    )(xb, res, wg, wu, wd, g, b)


def _qkv_rope_kernel(x_ref, w_ref, c_ref, s1_ref, s2_ref, o_ref):
    j = pl.program_id(1)
    y = jnp.dot(x_ref[...], w_ref[...], preferred_element_type=F32)

    @pl.when(j % 3 == 2)
    def _():
        o_ref[...] = y

    @pl.when(j % 3 != 2)
    def _():
        c, s1, s2 = c_ref[...], s1_ref[...], s2_ref[...]
        for h in range(DIL_HEADS):
            yh = y[:, h * DIL_HEAD_DIM:(h + 1) * DIL_HEAD_DIM]
            o_ref[:, h * DIL_HEAD_DIM:(h + 1) * DIL_HEAD_DIM] = (
                yh * c + pltpu.roll(yh, ROPE_DIM // 2, 1) * s1
                + pltpu.roll(yh, DIL_HEAD_DIM - ROPE_DIM // 2, 1) * s2)


def _qkv_rope(xb, w, cos_t, sin_lo, sin_hi):
    m, d = xb.shape
    n = w.shape[1]
    tn = DIL_HEADS * DIL_HEAD_DIM
    tm = ROW_TILE
    tab = pl.BlockSpec((tm, DIL_HEAD_DIM), lambda i, j: (i, 0))
    return pl.pallas_call(
        _qkv_rope_kernel,
        grid=(m // tm, n // tn),
        in_specs=[pl.BlockSpec((tm, d), lambda i, j: (i, 0)),
                  pl.BlockSpec((d, tn), lambda i, j: (0, j)),
                  tab, tab, tab],
        out_specs=pl.BlockSpec((tm, tn), lambda i, j: (i, j)),
        out_shape=jax.ShapeDtypeStruct((m, n), F32),
        compiler_params=_cparams(("parallel", "arbitrary")),
        name="dil_qkv_rope",
    )(xb, w, cos_t, sin_lo, sin_hi)


def _rope_tables(pos):
    half = ROPE_DIM // 2
    inv = jnp.power(ROPE_THETA, -jnp.arange(half, dtype=F32) / half)
    ang = pos.astype(F32)[:, None] * inv[None, :]
    cos, sin = jnp.cos(ang), jnp.sin(ang)
    rest = DIL_HEAD_DIM - ROPE_DIM
    n = pos.shape[0]
    c = jnp.concatenate([cos, cos, jnp.ones((n, rest), F32)], axis=1)
    s1 = jnp.concatenate([jnp.zeros((n, half), F32), sin, jnp.zeros((n, rest), F32)], axis=1)
    s2 = jnp.concatenate([-sin, jnp.zeros((n, half + rest), F32)], axis=1)
    return c, s1, s2


def _merge_groups(outs, lses):
    m = functools.reduce(jnp.maximum, lses)
    ws = [jnp.exp(l - m) for l in lses]
    num = functools.reduce(lambda a, b: a + b, [w * o for w, o in zip(ws, outs)])
    den = functools.reduce(lambda a, b: a + b, ws)
    return num / den


def _dil_prompt_kernel(*refs, seq_len):
    n_g = len(DIL_GROUPS)
    qkv_refs = refs[:3 * n_g]
    o_ref = refs[3 * n_g]
    qs_ref, ks_ref, vs_ref, osub_ref, lsub_ref = refs[3 * n_g + 1:3 * n_g + 6]
    otok_refs = refs[3 * n_g + 6:3 * n_g + 6 + n_g]
    ltok_refs = refs[3 * n_g + 6 + n_g:]
    blk = DIL_BLOCK
    n_blk = seq_len // blk
    scale = DIL_HEAD_DIM ** -0.5
    nt = (((1,), (1,)), ((), ()))
    qi = lax.broadcasted_iota(jnp.int32, (blk, blk), 0)
    kj = lax.broadcasted_iota(jnp.int32, (blk, blk), 1)
    neg = -jnp.inf

    for g, (window, dil) in enumerate(DIL_GROUPS):
        assert window // dil == blk
        q_ref, k_ref, v_ref = qkv_refs[3 * g:3 * g + 3]
        sub_len = seq_len // dil
        if dil > 1:
            for r in range(dil):
                dst = pl.ds(r * sub_len, sub_len)
                src = pl.ds(r, sub_len, stride=dil)
                qs_ref[dst, :] = q_ref[src, :]
                ks_ref[dst, :] = k_ref[src, :]
                vs_ref[dst, :] = v_ref[src, :]
            q_src, k_src, v_src, o_dst, l_dst = qs_ref, ks_ref, vs_ref, osub_ref, lsub_ref
        else:
            q_src, k_src, v_src, o_dst, l_dst = q_ref, k_ref, v_ref, otok_refs[g], ltok_refs[g]
        blk_per_sub = sub_len // blk

        def block(i, carry, q_src=q_src, k_src=k_src, v_src=v_src, o_dst=o_dst, l_dst=l_dst,
                  blk_per_sub=blk_per_sub):
            has_prev = (i % blk_per_sub) != 0
            cur = pl.ds(pl.multiple_of(i * blk, blk), blk)
            prev = pl.ds(pl.multiple_of(jnp.maximum(i - 1, 0) * blk, blk), blk)
            q = (q_src[cur, :] * scale).astype(BF16)
            s_cur = lax.dot_general(q, k_src[cur, :].astype(BF16), nt, preferred_element_type=F32)
            s_prev = lax.dot_general(q, k_src[prev, :].astype(BF16), nt, preferred_element_type=F32)
            s_cur = jnp.where(kj <= qi, s_cur, neg)
            s_prev = jnp.where((kj >= qi) & has_prev, s_prev, neg)
            m = jnp.maximum(jnp.max(s_cur, axis=-1, keepdims=True), jnp.max(s_prev, axis=-1, keepdims=True))
            p_cur = jnp.exp(s_cur - m)
            p_prev = jnp.exp(s_prev - m)
            den = jnp.sum(p_cur, axis=-1, keepdims=True) + jnp.sum(p_prev, axis=-1, keepdims=True)
            o = (jnp.dot(p_cur.astype(BF16), v_src[cur, :].astype(BF16), preferred_element_type=F32)
                 + jnp.dot(p_prev.astype(BF16), v_src[prev, :].astype(BF16), preferred_element_type=F32))
            o_dst[cur, :] = o / den
            l_dst[cur, :] = jnp.broadcast_to(m + jnp.log(den), (blk, DIL_HEAD_DIM))
            return carry

        lax.fori_loop(0, n_blk, block, 0)

        if dil > 1:
            for r in range(dil):
                src = pl.ds(r * sub_len, sub_len)
                dst = pl.ds(r, sub_len, stride=dil)
                otok_refs[g][dst, :] = osub_ref[src, :]
                ltok_refs[g][dst, :] = lsub_ref[src, :]

    def merge(i, carry):
        rows = pl.ds(pl.multiple_of(i * blk, blk), blk)
        o_ref[rows, :] = _merge_groups([r[rows, :] for r in otok_refs], [r[rows, :] for r in ltok_refs])
        return carry

    lax.fori_loop(0, n_blk, merge, 0)


def _dil_prompt_attention(qkv, n_seq, seq_len):
    n_g = len(DIL_GROUPS)
    hd = DIL_HEAD_DIM
    in_specs = []
    for g in range(n_g):
        for part in range(3):
            col0 = (g * 3 + part) * DIL_HEADS
            in_specs.append(pl.BlockSpec((seq_len, hd), lambda n, h, col0=col0: (n, col0 + h)))
    blk2d = pltpu.VMEM((seq_len, hd), F32)
    return pl.pallas_call(
        functools.partial(_dil_prompt_kernel, seq_len=seq_len),
        grid=(n_seq, DIL_HEADS),
        in_specs=in_specs,
        out_specs=pl.BlockSpec((seq_len, hd), lambda n, h: (n, h)),
        out_shape=jax.ShapeDtypeStruct((n_seq * seq_len, DIL_HEADS * hd), F32),
        scratch_shapes=[blk2d] * (5 + 2 * n_g),
        compiler_params=_cparams(("parallel", "parallel")),
        name="dil_prompt_attention",
    )(*([qkv] * (3 * n_g)))


def _dil_sample_kernel(new_ref, c0_ref, c1_ref, c2_ref, o_ref):
    scale = DIL_HEAD_DIM ** -0.5
    outs, lses = [], []
    for g, c_ref in enumerate((c0_ref, c1_ref, c2_ref)):
        q = new_ref[3 * g] * scale
        k_new = new_ref[3 * g + 1]
        v_new = new_ref[3 * g + 2]
        k_c = c_ref[:, 0]
        v_c = c_ref[:, 1]
        s_c = jnp.sum(k_c * q[None], axis=-1, keepdims=True)
        s_n = jnp.sum(k_new * q, axis=-1, keepdims=True)
        m = jnp.maximum(jnp.max(s_c, axis=0), s_n)
        p_c = jnp.exp(s_c - m[None])
        p_n = jnp.exp(s_n - m)
        den = jnp.sum(p_c, axis=0) + p_n
        o = (jnp.sum(p_c * v_c, axis=0) + p_n * v_new) / den
        outs.append(o)
        lses.append(jnp.broadcast_to(m + jnp.log(den), o.shape))
    o_ref[...] = _merge_groups(outs, lses)


def _dil_sample_attention(new_qkv, caches):
    n = new_qkv.shape[0]
    h, hd = DIL_HEADS, DIL_HEAD_DIM
    cache_specs, cache_views = [], []
    for (window, dil), c in zip(DIL_GROUPS, caches):
        assert c.shape[1] == window, "cache must hold one full window"
        n_keys = window // dil
        cache_views.append(c.reshape(n, n_keys, dil, 2, h, hd))
        cache_specs.append(pl.BlockSpec((None, n_keys, None, 2, h, hd), lambda i: (i, 0, 0, 0, 0, 0)))
    return pl.pallas_call(
        _dil_sample_kernel,
        grid=(n,),
        in_specs=[pl.BlockSpec((None, 9, h, hd), lambda i: (i, 0, 0, 0))] + cache_specs,
        out_specs=pl.BlockSpec((None, h, hd), lambda i: (i, 0, 0)),
        out_shape=jax.ShapeDtypeStruct((n, h, hd), F32),
        compiler_params=_cparams(("parallel",)),
        name="dil_sample_attention",
    )(new_qkv, *cache_views)


def _router_kernel(x_ref, wh_ref, wl_ref, idx_ref, gate_ref):
    x = x_ref[...]
    xh = x.astype(BF16)
    xl = (x - xh.astype(F32)).astype(BF16)
    wh, wl = wh_ref[...], wl_ref[...]
    logits = (jnp.dot(xh, wh, preferred_element_type=F32) + jnp.dot(xl, wh, preferred_element_type=F32)
              + jnp.dot(xh, wl, preferred_element_type=F32))
    lane = lax.broadcasted_iota(jnp.int32, logits.shape, 1)
    neg = -jnp.inf
    lg = jnp.where(lane < N_EXPERTS, logits, neg)
    v1 = jnp.max(lg, axis=-1, keepdims=True)
    i1 = jnp.min(jnp.where(lg == v1, lane, LANES), axis=-1, keepdims=True)
    lg2 = jnp.where(lane == i1, neg, lg)
    v2 = jnp.max(lg2, axis=-1, keepdims=True)
    i2 = jnp.min(jnp.where(lg2 == v2, lane, LANES), axis=-1, keepdims=True)
    e = jnp.exp(v2 - v1)
    g1 = 1.0 / (1.0 + e)
    g2 = e / (1.0 + e)
    idx_ref[...] = jnp.where(lane == 0, i1, jnp.where(lane == 1, i2, 0))
    gate_ref[...] = jnp.where(lane == 0, g1, jnp.where(lane == 1, g2, 0.0))


def _router(x, w_hi, w_lo):
    m, d = x.shape
    tm = ROW_TILE
    return pl.pallas_call(
        _router_kernel,
        grid=(m // tm,),
        in_specs=[pl.BlockSpec((tm, d), lambda i: (i, 0)),
                  pl.BlockSpec((d, LANES), lambda i: (0, 0)),
                  pl.BlockSpec((d, LANES), lambda i: (0, 0))],
        out_specs=[pl.BlockSpec((tm, LANES), lambda i: (i, 0)),
                   pl.BlockSpec((tm, LANES), lambda i: (i, 0))],
        out_shape=[jax.ShapeDtypeStruct((m, LANES), jnp.int32), jax.ShapeDtypeStruct((m, LANES), F32)],
        compiler_params=_cparams(("parallel",)),
        name="moe_router",
    )(x, w_hi, w_lo)


def _row_copy(src_hbm, dst_vmem, sem, src_row, dst_row):
    return pltpu.make_async_copy(src_hbm.at[pl.ds(src_row, 1)], dst_vmem.at[pl.ds(dst_row, 1)], sem)


def _moe_kernel(be_ref, bv_ref, tok_ref, x_hbm, wg_ref, wu_ref, wd_ref, y_ref,
                xrow_ref, xb_ref, acc_ref, sem):
    b = pl.program_id(0)
    f = pl.program_id(1)
    tm = xrow_ref.shape[0]

    @pl.when(bv_ref[b] == 1)
    def _():
        @pl.when(f == 0)
        def _():
            def start(i, c):
                _row_copy(x_hbm, xrow_ref, sem, tok_ref[0, i], i).start()
                return c

            def wait(i, c):
                _row_copy(x_hbm, xrow_ref, sem, tok_ref[0, i], i).wait()
                return c

            lax.fori_loop(0, tm, start, 0)
            lax.fori_loop(0, tm, wait, 0)
            xb_ref[...] = xrow_ref[...].astype(BF16)
            acc_ref[...] = jnp.zeros_like(acc_ref)

        x = xb_ref[...]
        gate = jnp.dot(x, wg_ref[...], preferred_element_type=F32)
        up = jnp.dot(x, wu_ref[...], preferred_element_type=F32)
        h = (_silu(gate) * up).astype(BF16)
        acc_ref[...] += jnp.dot(h, wd_ref[...], preferred_element_type=F32)

        @pl.when(f == pl.num_programs(1) - 1)
        def _():
            y_ref[...] = acc_ref[...]

    @pl.when((bv_ref[b] == 0) & (f == 0))
    def _():
        y_ref[...] = jnp.zeros_like(y_ref)


def _moe_experts(x, row_tok, block_expert, block_valid, last_valid, wg, wu, wd, tm, tf):
    d = x.shape[1]
    d_ff = wg.shape[2]
    n_blocks = row_tok.shape[0]
    nf = d_ff // tf

    def f_idx(b, f, bv):
        return jnp.where(bv[b] == 1, f, nf - 1)

    def b_idx(b, lv):
        return jnp.minimum(b, lv[0])

    grid_spec = pltpu.PrefetchScalarGridSpec(
        num_scalar_prefetch=3,
        grid=(n_blocks, nf),
        in_specs=[pl.BlockSpec((None, 1, tm), lambda b, f, be, bv, lv: (b_idx(b, lv), 0, 0),
                               memory_space=pltpu.SMEM),
                  pl.BlockSpec(memory_space=pl.ANY),
                  pl.BlockSpec((None, d, tf), lambda b, f, be, bv, lv: (be[b], 0, f_idx(b, f, bv))),
                  pl.BlockSpec((None, d, tf), lambda b, f, be, bv, lv: (be[b], 0, f_idx(b, f, bv))),
                  pl.BlockSpec((None, tf, d), lambda b, f, be, bv, lv: (be[b], f_idx(b, f, bv), 0))],
        out_specs=pl.BlockSpec((tm, d), lambda b, f, be, bv, lv: (b, 0)),
        scratch_shapes=[pltpu.VMEM((tm, d), F32), pltpu.VMEM((tm, d), BF16), pltpu.VMEM((tm, d), F32),
                        pltpu.SemaphoreType.DMA(())],
    )

    def kern(be_ref, bv_ref, lv_ref, tok_ref, x_hbm, wg_ref, wu_ref, wd_ref, y_ref, xrow_ref, xb_ref, acc_ref, sem):
        _moe_kernel(be_ref, bv_ref, tok_ref, x_hbm, wg_ref, wu_ref, wd_ref, y_ref, xrow_ref, xb_ref, acc_ref, sem)

    return pl.pallas_call(
        kern,
        grid_spec=grid_spec,
        out_shape=jax.ShapeDtypeStruct((n_blocks * tm, d), F32),
        compiler_params=_cparams(("arbitrary", "arbitrary")),
        name="moe_experts",
    )(block_expert, block_valid, last_valid, row_tok, x, wg, wu, wd)


def _combine_kernel(pos_ref, y_hbm, res_ref, gate_ref, g_ref, b_ref, o_ref, y1_ref, y2_ref, sem):
    tm = res_ref.shape[0]

    def start(i, c):
        _row_copy(y_hbm, y1_ref, sem.at[0], pos_ref[0, i], i).start()
        _row_copy(y_hbm, y2_ref, sem.at[1], pos_ref[1, i], i).start()
        return c

    def wait(i, c):
        _row_copy(y_hbm, y1_ref, sem.at[0], pos_ref[0, i], i).wait()
        _row_copy(y_hbm, y2_ref, sem.at[1], pos_ref[1, i], i).wait()
        return c

    lax.fori_loop(0, tm, start, 0)
    lax.fori_loop(0, tm, wait, 0)
    gates = gate_ref[...]
    y = y1_ref[...] * gates[:, 0:1] + y2_ref[...] * gates[:, 1:2]
    o_ref[...] = _layer_norm(DEEPNORM_ALPHA * res_ref[...] + y, g_ref[...], b_ref[...])


def _moe_combine_ln(pos, y_rows, res, gates, g, b):
    m, d = res.shape
    tm = pos.shape[2]
    return pl.pallas_call(
        _combine_kernel,
        grid=(m // tm,),
        in_specs=[pl.BlockSpec((None, 2, tm), lambda i: (i, 0, 0), memory_space=pltpu.SMEM),
                  pl.BlockSpec(memory_space=pl.ANY),
                  pl.BlockSpec((tm, d), lambda i: (i, 0)),
                  pl.BlockSpec((tm, LANES), lambda i: (i, 0)),
                  pl.BlockSpec((1, d), lambda i: (0, 0)),
                  pl.BlockSpec((1, d), lambda i: (0, 0))],
        out_specs=pl.BlockSpec((tm, d), lambda i: (i, 0)),
        out_shape=jax.ShapeDtypeStruct((m, d), F32),
        scratch_shapes=[pltpu.VMEM((tm, d), F32), pltpu.VMEM((tm, d), F32), pltpu.SemaphoreType.DMA((2,))],
        compiler_params=_cparams(("arbitrary",)),
        name="moe_combine_ln",
    )(pos, y_rows, res, gates, g, b)


def _route_rows(idx, gates, n_tok, m_pad, tm):
    n_pairs = n_tok * TOP_K
    flat_e = idx[:n_tok, :TOP_K].reshape(-1)
    order = jnp.argsort(flat_e, stable=True)
    e_sorted = flat_e[order]
    counts = jnp.bincount(flat_e, length=N_EXPERTS)
    padded = (counts + tm - 1) // tm * tm
    start = jnp.cumsum(counts) - counts
    pend = jnp.cumsum(padded)
    pstart = pend - padded
    dest = (pstart[e_sorted] + jnp.arange(n_pairs) - start[e_sorted]).astype(jnp.int32)
    n_blocks = -(-n_pairs // tm) + N_EXPERTS
    row_tok = jnp.zeros((n_blocks * tm,), jnp.int32).at[dest].set((order // TOP_K).astype(jnp.int32))
    pos_flat = jnp.zeros((m_pad * TOP_K,), jnp.int32).at[order].set(dest)
    blk_start = jnp.arange(n_blocks) * tm
    block_expert = jnp.clip(jnp.searchsorted(pend, blk_start, side='right'), 0, N_EXPERTS - 1).astype(jnp.int32)
    block_valid = (blk_start < pend[-1]).astype(jnp.int32)
    last_valid = (pend[-1] // tm - 1).astype(jnp.int32).reshape(1)
    block_expert = jnp.where(block_valid == 1, block_expert, block_expert[last_valid[0]])
    pos = pos_flat.reshape(m_pad // tm, tm, TOP_K).transpose(0, 2, 1)
    return row_tok.reshape(n_blocks, 1, tm), pos, block_expert, block_valid, last_valid


def kernel(x_prompt, x_sample, state_gla, cache_win128, cache_win512, cache_win2048, gla_w_in, gla_w_gate_up, gla_b_gate, gla_norm_g, gla_w_out, ln_mix0_g, ln_mix0_b, ffn_w_gate, ffn_w_up, ffn_w_down, ln_ffn0_g, ln_ffn0_b, dil_w_in, dil_w_out, ln_mix1_g, ln_mix1_b, moe_w_router, moe_w_gate, moe_w_up, moe_w_down, ln_ffn1_g, ln_ffn1_b):
    n_p, seq_len, d = x_prompt.shape
    n_s, dec_len, _ = x_sample.shape
    assert dec_len == 1 and seq_len % 2048 == 0
    rows_p, rows_s = n_p * seq_len, n_s * dec_len
    n_tok = rows_p + rows_s
    m_pad = -(-n_tok // ROW_TILE) * ROW_TILE
    row = lambda a: a.reshape(1, -1)

    x0 = jnp.concatenate([x_prompt.reshape(rows_p, d), x_sample.reshape(rows_s, d),
                          jnp.zeros((m_pad - n_tok, d), F32)], axis=0)
    x0b = x0.astype(BF16)

    dk_tot = gla_w_gate_up.shape[1]
    n_main = gla_w_in.shape[1] - GLA_GATE_RANK
    proj = _matmul(x0b, gla_w_in[:, :n_main].astype(BF16), 512)
    w_low = jnp.pad(gla_w_in[:, n_main:], ((0, 0), (0, LANES - GLA_GATE_RANK))).astype(BF16)
    w_up = jnp.pad(gla_w_gate_up, ((0, LANES - GLA_GATE_RANK), (0, 0))).astype(BF16)
    log_a = _gla_gate(x0b, w_low, w_up, row(gla_b_gate))

    s_zero = jnp.zeros((n_p,) + state_gla.shape[1:], F32)
    og_p, state_p = _gla_recurrence(proj, log_a, s_zero, row(gla_norm_g), n_p, seq_len, 256)
    pad_t = lambda a: jnp.pad(a[rows_p:n_tok].reshape(n_s, 1, -1),
                              ((0, 0), (0, GLA_CHUNK - 1), (0, 0))).reshape(n_s * GLA_CHUNK, -1)
    og_s, state_s = _gla_recurrence(pad_t(proj), pad_t(log_a), state_gla, row(gla_norm_g), n_s, GLA_CHUNK, GLA_CHUNK)
    og = jnp.concatenate([og_p, og_s[::GLA_CHUNK], jnp.zeros((m_pad - n_tok, og_p.shape[1]), F32)], axis=0)
    x1, x1b = _proj_ln(og, gla_w_out.astype(BF16), x0, row(ln_mix0_g), row(ln_mix0_b), "gla_out_ln")

    x2, x2b = _ffn_ln(x1b, x1, ffn_w_gate.astype(BF16), ffn_w_up.astype(BF16), ffn_w_down.astype(BF16),
                      row(ln_ffn0_g), row(ln_ffn0_b), 512)

    pos = jnp.concatenate([jnp.tile(jnp.arange(seq_len, dtype=jnp.int32), n_p),
                           jnp.tile(PAST_LEN + jnp.arange(dec_len, dtype=jnp.int32), n_s),
                           jnp.zeros((m_pad - n_tok,), jnp.int32)])
    qkv = _qkv_rope(x2b, dil_w_in.astype(BF16), *_rope_tables(pos))
    att_p = _dil_prompt_attention(qkv, n_p, seq_len)
    n_g = len(DIL_GROUPS)
    new_qkv = qkv[rows_p:n_tok].reshape(n_s, 3 * n_g, DIL_HEADS, DIL_HEAD_DIM)
    caches = (cache_win128, cache_win512, cache_win2048)
    att_s = _dil_sample_attention(new_qkv, caches)
    att = jnp.concatenate([att_p, att_s.reshape(rows_s, -1), jnp.zeros((m_pad - n_tok, att_p.shape[1]), F32)], axis=0)
    x3, _ = _proj_ln(att, dil_w_out.astype(BF16), x2, row(ln_mix1_g), row(ln_mix1_b), "dil_out_ln")

    wins_p, wins_s = [], []
    for g, (window, _) in enumerate(DIL_GROUPS):
        width = 2 * DIL_HEADS * DIL_HEAD_DIM
        kv = qkv[:, (3 * g + 1) * width // 2:(3 * g + 3) * width // 2]
        kv_p = kv[:rows_p].reshape(n_p, seq_len, 2, DIL_HEADS, DIL_HEAD_DIM)
        wins_p.append(kv_p[:, seq_len - min(window, seq_len):])
        kv_s = kv[rows_p:n_tok].reshape(n_s, dec_len, 2, DIL_HEADS, DIL_HEAD_DIM)
        wins_s.append(jnp.concatenate([caches[g], kv_s], axis=1)[:, dec_len:])

    w_r = jnp.pad(moe_w_router, ((0, 0), (0, LANES - N_EXPERTS)))
    w_r_hi = w_r.astype(BF16)
    w_r_lo = (w_r - w_r_hi.astype(F32)).astype(BF16)
    idx, gates = _router(x3, w_r_hi, w_r_lo)
    row_tok, pos_rows, block_expert, block_valid, last_valid = _route_rows(idx, gates, n_tok, m_pad, ROW_TILE)
    y_rows = _moe_experts(x3, row_tok, block_expert, block_valid, last_valid,
                          moe_w_gate.astype(BF16), moe_w_up.astype(BF16), moe_w_down.astype(BF16), ROW_TILE, 512)
    x4 = _moe_combine_ln(pos_rows, y_rows, x3, gates, row(ln_ffn1_g), row(ln_ffn1_b))

    y_p = x4[:rows_p].reshape(n_p, seq_len, d)
    y_s = x4[rows_p:n_tok].reshape(n_s, dec_len, d)
    return (y_p, y_s, state_p, wins_p[0], wins_p[1], wins_p[2], state_s, wins_s[0], wins_s[1], wins_s[2])
```

```python
import functools

import jax
import jax.numpy as jnp
from jax import lax
from jax.experimental import pallas as pl
from jax.experimental.pallas import tpu as pltpu

F32 = jnp.float32
BF16 = jnp.bfloat16

GLA_HEADS = 4
GLA_GATE_RANK = 16
GLA_GATE_NORM = 16.0
GLA_CHUNK = 32
DIL_GROUPS = ((128, 1), (512, 4), (2048, 16))
DIL_HEADS = 8
DIL_HEAD_DIM = 128
DIL_BLOCK = 128
ROPE_THETA = 500000.0
ROPE_DIM = DIL_HEAD_DIM // 4
N_EXPERTS = 8
TOP_K = 2
LN_EPS = 1e-5
RMS_EPS = 1e-6
DEPTH = 2
DEEPNORM_ALPHA = (2 * DEPTH) ** 0.25
PAST_LEN = 8192

LANES = 128
BF16_SUBLANES = 16
VMEM_LIMIT = 56 * 1024 * 1024
MM_ROWS = 1024
FFN_ROWS = 640
LN_ROWS = 448
MOE_ROWS = 512
GLA_T_BLOCK = 256


def _row_tile(m, target):
    best = None
    for t in range(BF16_SUBLANES, min(m, target) + 1, BF16_SUBLANES):
        if m % t == 0:
            best = t
    assert best is not None, (m, target)
    return best


def _cparams(sem):
    return pltpu.CompilerParams(dimension_semantics=sem, vmem_limit_bytes=VMEM_LIMIT)


def _layer_norm(z, g, b):
    mu = jnp.mean(z, axis=-1, keepdims=True)
    zc = z - mu
    var = jnp.mean(zc * zc, axis=-1, keepdims=True)
    return zc * lax.rsqrt(var + LN_EPS) * g + b


def _silu(x):
    return x * (1.0 / (1.0 + jnp.exp(-x)))


def _mm_kernel(x_ref, w_ref, o_ref):
    o_ref[...] = jnp.dot(x_ref[...], w_ref[...], preferred_element_type=F32)


def _matmul(x, w, tn):
    m, k = x.shape
    n = w.shape[1]
    tm = _row_tile(m, MM_ROWS)
    return pl.pallas_call(
        _mm_kernel,
        grid=(m // tm, n // tn),
        in_specs=[pl.BlockSpec((tm, k), lambda i, j: (i, 0)),
                  pl.BlockSpec((k, tn), lambda i, j: (0, j))],
        out_specs=pl.BlockSpec((tm, tn), lambda i, j: (i, j)),
        out_shape=jax.ShapeDtypeStruct((m, n), F32),
        compiler_params=_cparams(("parallel", "arbitrary")),
        name="gla_in_proj",
    )(x, w)


def _gate_kernel(x_ref, wl_ref, wu_ref, b_ref, o_ref):
    g_low = jnp.dot(x_ref[...], wl_ref[...], preferred_element_type=F32)
    z = jnp.dot(g_low.astype(BF16), wu_ref[...], preferred_element_type=F32) + b_ref[...]
    log_sig = jnp.minimum(z, 0.0) - jnp.log1p(jnp.exp(-jnp.abs(z)))
    o_ref[...] = log_sig * (1.0 / GLA_GATE_NORM)


def _gla_gate(xb, w_low, w_up, b_gate):
    m, d = xb.shape
    dk = w_up.shape[1]
    tm = _row_tile(m, MM_ROWS)
    return pl.pallas_call(
        _gate_kernel,
        grid=(m // tm,),
        in_specs=[pl.BlockSpec((tm, d), lambda i: (i, 0)),
                  pl.BlockSpec((d, LANES), lambda i: (0, 0)),
                  pl.BlockSpec((LANES, dk), lambda i: (0, 0)),
                  pl.BlockSpec((1, dk), lambda i: (0, 0))],
        out_specs=pl.BlockSpec((tm, dk), lambda i: (i, 0)),
        out_shape=jax.ShapeDtypeStruct((m, dk), F32),
        compiler_params=_cparams(("parallel",)),
        name="gla_gate",
    )(xb, w_low, w_up, b_gate)


def _split3(x):
    x1 = x.astype(BF16)
    r1 = x - x1.astype(F32)
    x2 = r1.astype(BF16)
    x3 = (r1 - x2.astype(F32)).astype(BF16)
    return x1, x2, x3


def _gla_rec_kernel(q_ref, k_ref, v_ref, r_ref, ga_ref, s0_ref, ng_ref,
                    og_ref, sout_ref, st_ref, *, n_chunks, head_k):
    t = pl.program_id(2)
    c_len = GLA_CHUNK
    t_blk = n_chunks * c_len

    @pl.when(t == 0)
    def _():
        st_ref[...] = s0_ref[...].T

    row = lax.broadcasted_iota(jnp.int32, (t_blk, t_blk), 0)
    col = lax.broadcasted_iota(jnp.int32, (t_blk, t_blk), 1)
    causal = (col <= row) & (col >= (row & -c_len))
    tril = jnp.where(causal, 1.0, 0.0).astype(BF16)
    scale = head_k ** -0.5
    nt = (((1,), (1,)), ((), ()))
    tn = (((0,), (0,)), ((), ()))

    g1, g2, g3 = _split3(ga_ref[...])
    b = (jnp.dot(tril, g1, preferred_element_type=F32)
         + jnp.dot(tril, g2, preferred_element_type=F32)
         + jnp.dot(tril, g3, preferred_element_type=F32))
    k = k_ref[...]
    v16 = v_ref[...].astype(BF16)
    q_e = (q_ref[...] * scale * jnp.exp(b)).astype(BF16)
    k_e = (k * jnp.exp(-b)).astype(BF16)
    a = lax.dot_general(q_e, k_e, nt, preferred_element_type=F32)
    a = jnp.where(causal, a, 0.0).astype(BF16)
    o_intra = jnp.dot(a, v16, preferred_element_type=F32)
    ng = ng_ref[...]

    for c in range(n_chunks):
        rows = slice(c * c_len, (c + 1) * c_len)
        b_c = b[rows]
        b_last = b_c[c_len - 1:c_len]
        k_dec = (k[rows] * jnp.exp(b_last - b_c)).astype(BF16)
        st = st_ref[...]
        o = lax.dot_general(q_e[rows], st.astype(BF16), nt, preferred_element_type=F32) + o_intra[rows]
        st_ref[...] = st * jnp.exp(b_last) + lax.dot_general(v16[rows], k_dec, tn, preferred_element_type=F32)
        o = o * lax.rsqrt(jnp.mean(o * o, axis=-1, keepdims=True) + RMS_EPS) * ng
        og_ref[rows, :] = (o * _silu(r_ref[rows, :])).astype(og_ref.dtype)

    @pl.when(t == pl.num_programs(2) - 1)
    def _():
        sout_ref[...] = st_ref[...].T


def _gla_recurrence(proj, log_a, s0, norm_g, n_seq, seq_len, t_blk):
    n_heads = GLA_HEADS
    dk = log_a.shape[1] // n_heads
    dv = s0.shape[3]
    kq_blocks = log_a.shape[1] // dk
    kv_off = 2 * log_a.shape[1] // dv
    nt = seq_len // t_blk
    rows = n_seq * seq_len
    row_map = lambda col: (lambda n, h, t: (n * nt + t, col(h)))
    kern = functools.partial(_gla_rec_kernel, n_chunks=t_blk // GLA_CHUNK, head_k=dk)
    return pl.pallas_call(
        kern,
        grid=(n_seq, n_heads, nt),
        in_specs=[pl.BlockSpec((t_blk, dk), row_map(lambda h: h)),
                  pl.BlockSpec((t_blk, dk), row_map(lambda h: kq_blocks + h)),
                  pl.BlockSpec((t_blk, dv), row_map(lambda h: kv_off + h)),
                  pl.BlockSpec((t_blk, dv), row_map(lambda h: kv_off + n_heads + h)),
                  pl.BlockSpec((t_blk, dk), row_map(lambda h: h)),
                  pl.BlockSpec((None, None, dk, dv), lambda n, h, t: (n, h, 0, 0)),
                  pl.BlockSpec((1, dv), lambda n, h, t: (0, 0))],
        out_specs=[pl.BlockSpec((t_blk, dv), row_map(lambda h: h)),
                   pl.BlockSpec((None, None, dk, dv), lambda n, h, t: (n, h, 0, 0))],
        out_shape=[jax.ShapeDtypeStruct((rows, n_heads * dv), BF16),
                   jax.ShapeDtypeStruct(s0.shape, F32)],
        scratch_shapes=[pltpu.VMEM((dv, dk), F32)],
        compiler_params=_cparams(("parallel", "parallel", "arbitrary")),
        name="gla_recurrence",
    )(proj, proj, proj, proj, log_a, s0, norm_g)


def _proj_ln_kernel(a_ref, w_ref, res_ref, g_ref, b_ref, o_ref):
    y = jnp.dot(a_ref[...], w_ref[...], preferred_element_type=F32)
    o_ref[...] = _layer_norm(DEEPNORM_ALPHA * res_ref[...] + y, g_ref[...], b_ref[...])


def _proj_ln(a, w, res, g, b, name):
    m, k = a.shape
    d = w.shape[1]
    tm = _row_tile(m, LN_ROWS)
    return pl.pallas_call(
        _proj_ln_kernel,
        grid=(m // tm,),
        in_specs=[pl.BlockSpec((tm, k), lambda i: (i, 0)),
                  pl.BlockSpec((k, d), lambda i: (0, 0)),
                  pl.BlockSpec((tm, d), lambda i: (i, 0)),
                  pl.BlockSpec((1, d), lambda i: (0, 0)),
                  pl.BlockSpec((1, d), lambda i: (0, 0))],
        out_specs=pl.BlockSpec((tm, d), lambda i: (i, 0)),
        out_shape=jax.ShapeDtypeStruct((m, d), F32),
        compiler_params=_cparams(("parallel",)),
        name=name,
    )(a, w, res, g, b)


def _ffn_kernel(x_ref, wg_ref, wu_ref, wd_ref, g_ref, b_ref, of_ref, ob_ref, xb_ref):
    f = pl.program_id(1)

    @pl.when(f == 0)
    def _():
        xb_ref[...] = x_ref[...].astype(BF16)
        of_ref[...] = jnp.zeros_like(of_ref)

    x = xb_ref[...]
    gate = jnp.dot(x, wg_ref[...], preferred_element_type=F32)
    up = jnp.dot(x, wu_ref[...], preferred_element_type=F32)
    h = (_silu(gate) * up).astype(BF16)
    of_ref[...] += jnp.dot(h, wd_ref[...], preferred_element_type=F32)

    @pl.when(f == pl.num_programs(1) - 1)
    def _():
        out = _layer_norm(DEEPNORM_ALPHA * x_ref[...] + of_ref[...], g_ref[...], b_ref[...])
        of_ref[...] = out
        ob_ref[...] = out.astype(BF16)


def _ffn_ln(x, wg, wu, wd, g, b, tf):
    m, d = x.shape
    d_ff = wg.shape[1]
    tm = _row_tile(m, FFN_ROWS)
    return pl.pallas_call(
        _ffn_kernel,
        grid=(m // tm, d_ff // tf),
        in_specs=[pl.BlockSpec((tm, d), lambda i, f: (i, 0)),
                  pl.BlockSpec((d, tf), lambda i, f: (0, f)),
                  pl.BlockSpec((d, tf), lambda i, f: (0, f)),
                  pl.BlockSpec((tf, d), lambda i, f: (f, 0)),
                  pl.BlockSpec((1, d), lambda i, f: (0, 0)),
                  pl.BlockSpec((1, d), lambda i, f: (0, 0))],
        out_specs=[pl.BlockSpec((tm, d), lambda i, f: (i, 0)),
                   pl.BlockSpec((tm, d), lambda i, f: (i, 0))],
        out_shape=[jax.ShapeDtypeStruct((m, d), F32), jax.ShapeDtypeStruct((m, d), BF16)],
        scratch_shapes=[pltpu.VMEM((tm, d), BF16)],
        compiler_params=_cparams(("parallel", "arbitrary")),
        name="ffn_swiglu_ln",
    )(x, wg, wu, wd, g, b)


def _qkv_rope_kernel(x_ref, w_ref, c_ref, s1_ref, s2_ref, o_ref):
    rot = (pl.program_id(1) % 3 != 2).astype(F32)
    c = 1.0 + (c_ref[...] - 1.0) * rot
    s1 = s1_ref[...] * rot
    s2 = s2_ref[...] * rot
    x = x_ref[...]
    hd = DIL_HEAD_DIM
    for h in range(0, DIL_HEADS, 2):
        y = jnp.dot(x, w_ref[:, h * hd:(h + 2) * hd], preferred_element_type=F32)
        for yh, lo in ((y[:, :hd], h * hd), (y[:, hd:], (h + 1) * hd)):
            o_ref[:, lo:lo + hd] = (yh * c + pltpu.roll(yh, ROPE_DIM // 2, 1) * s1
                                    + pltpu.roll(yh, hd - ROPE_DIM // 2, 1) * s2)


def _qkv_rope(xb, w, cos_t, sin_lo, sin_hi):
    m, d = xb.shape
    n = w.shape[1]
    tn = DIL_HEADS * DIL_HEAD_DIM
    tm = _row_tile(m, MM_ROWS)
    tab = pl.BlockSpec((tm, DIL_HEAD_DIM), lambda i, j: (i, 0))
    return pl.pallas_call(
        _qkv_rope_kernel,
        grid=(m // tm, n // tn),
        in_specs=[pl.BlockSpec((tm, d), lambda i, j: (i, 0)),
                  pl.BlockSpec((d, tn), lambda i, j: (0, j)),
                  tab, tab, tab],
        out_specs=pl.BlockSpec((tm, tn), lambda i, j: (i, j)),
        out_shape=jax.ShapeDtypeStruct((m, n), F32),
        compiler_params=_cparams(("parallel", "arbitrary")),
        name="dil_qkv_rope",
    )(xb, w, cos_t, sin_lo, sin_hi)


def _rope_tables(pos):
    half = ROPE_DIM // 2
    inv = jnp.power(ROPE_THETA, -jnp.arange(half, dtype=F32) / half)
    ang = pos.astype(F32)[:, None] * inv[None, :]
    cos, sin = jnp.cos(ang), jnp.sin(ang)
    rest = DIL_HEAD_DIM - ROPE_DIM
    n = pos.shape[0]
    c = jnp.concatenate([cos, cos, jnp.ones((n, rest), F32)], axis=1)
    s1 = jnp.concatenate([jnp.zeros((n, half), F32), sin, jnp.zeros((n, rest), F32)], axis=1)
    s2 = jnp.concatenate([-sin, jnp.zeros((n, half + rest), F32)], axis=1)
    return c, s1, s2


def _merge_groups(outs, lses):
    m = functools.reduce(jnp.maximum, lses)
    ws = [jnp.exp(l - m) for l in lses]
    num = functools.reduce(lambda a, b: a + b, [w * o for w, o in zip(ws, outs)])
    den = functools.reduce(lambda a, b: a + b, ws)
    return num / den


def _dil_prompt_kernel(*refs, seq_len):
    n_g = len(DIL_GROUPS)
    qkv_refs = refs[:3 * n_g]
    o_ref = refs[3 * n_g]
    qs_ref, ks_ref, vs_ref, osub_ref, lsub_ref = refs[3 * n_g + 1:3 * n_g + 6]
    otok_refs = refs[3 * n_g + 6:3 * n_g + 6 + n_g]
    ltok_refs = refs[3 * n_g + 6 + n_g:]
    blk = DIL_BLOCK
    n_blk = seq_len // blk
    scale = DIL_HEAD_DIM ** -0.5
    nt = (((1,), (1,)), ((), ()))
    qi = lax.broadcasted_iota(jnp.int32, (blk, blk), 0)
    kj = lax.broadcasted_iota(jnp.int32, (blk, blk), 1)
    neg = -jnp.inf

    for g, (window, dil) in enumerate(DIL_GROUPS):
        assert window // dil == blk
        q_ref, k_ref, v_ref = qkv_refs[3 * g:3 * g + 3]
        sub_len = seq_len // dil
        if dil > 1:
            for r in range(dil):
                dst = pl.ds(r * sub_len, sub_len)
                src = pl.ds(r, sub_len, stride=dil)
                qs_ref[dst, :] = q_ref[src, :]
                ks_ref[dst, :] = k_ref[src, :]
                vs_ref[dst, :] = v_ref[src, :]
            q_src, k_src, v_src, o_dst, l_dst = qs_ref, ks_ref, vs_ref, osub_ref, lsub_ref
        else:
            q_src, k_src, v_src, o_dst, l_dst = q_ref, k_ref, v_ref, otok_refs[g], ltok_refs[g]
        blk_per_sub = sub_len // blk

        def block(i, carry, q_src=q_src, k_src=k_src, v_src=v_src, o_dst=o_dst, l_dst=l_dst,
                  blk_per_sub=blk_per_sub):
            has_prev = (i % blk_per_sub) != 0
            cur = pl.ds(pl.multiple_of(i * blk, blk), blk)
            prev = pl.ds(pl.multiple_of(jnp.maximum(i - 1, 0) * blk, blk), blk)
            q = (q_src[cur, :] * scale).astype(BF16)
            s_cur = lax.dot_general(q, k_src[cur, :].astype(BF16), nt, preferred_element_type=F32)
            s_prev = lax.dot_general(q, k_src[prev, :].astype(BF16), nt, preferred_element_type=F32)
            s_cur = jnp.where(kj <= qi, s_cur, neg)
            s_prev = jnp.where((kj >= qi) & has_prev, s_prev, neg)
            m = jnp.maximum(jnp.max(s_cur, axis=-1, keepdims=True), jnp.max(s_prev, axis=-1, keepdims=True))
            p_cur = jnp.exp(s_cur - m)
            p_prev = jnp.exp(s_prev - m)
            den = jnp.sum(p_cur, axis=-1, keepdims=True) + jnp.sum(p_prev, axis=-1, keepdims=True)
            o = (jnp.dot(p_cur.astype(BF16), v_src[cur, :].astype(BF16), preferred_element_type=F32)
                 + jnp.dot(p_prev.astype(BF16), v_src[prev, :].astype(BF16), preferred_element_type=F32))
            o_dst[cur, :] = o / den
            l_dst[cur, :] = jnp.broadcast_to(m + jnp.log(den), (blk, DIL_HEAD_DIM))
            return carry

        lax.fori_loop(0, n_blk, block, 0, unroll=4)

        if dil > 1:
            for r in range(dil):
                src = pl.ds(r * sub_len, sub_len)
                dst = pl.ds(r, sub_len, stride=dil)
                otok_refs[g][dst, :] = osub_ref[src, :]
                ltok_refs[g][dst, :] = lsub_ref[src, :]

    def merge(i, carry):
        rows = pl.ds(pl.multiple_of(i * blk, blk), blk)
        o_ref[rows, :] = _merge_groups([r[rows, :] for r in otok_refs],
                                       [r[rows, :] for r in ltok_refs]).astype(o_ref.dtype)
        return carry

    lax.fori_loop(0, n_blk, merge, 0, unroll=2)


def _dil_prompt_attention(qkv, n_seq, seq_len):
    n_g = len(DIL_GROUPS)
    hd = DIL_HEAD_DIM
    in_specs = []
    for g in range(n_g):
        for part in range(3):
            col0 = (g * 3 + part) * DIL_HEADS
            in_specs.append(pl.BlockSpec((seq_len, hd), lambda n, h, col0=col0: (n, col0 + h)))
    blk2d = pltpu.VMEM((seq_len, hd), F32)
    return pl.pallas_call(
        functools.partial(_dil_prompt_kernel, seq_len=seq_len),
        grid=(n_seq, DIL_HEADS),
        in_specs=in_specs,
        out_specs=pl.BlockSpec((seq_len, hd), lambda n, h: (n, h)),
        out_shape=jax.ShapeDtypeStruct((n_seq * seq_len, DIL_HEADS * hd), BF16),
        scratch_shapes=[blk2d] * (5 + 2 * n_g),
        compiler_params=_cparams(("parallel", "parallel")),
        name="dil_prompt_attention",
    )(*([qkv] * (3 * n_g)))


def _dil_sample_kernel(new_ref, c0_ref, c1_ref, c2_ref, o_ref):
    scale = DIL_HEAD_DIM ** -0.5
    outs, lses = [], []
    for g, c_ref in enumerate((c0_ref, c1_ref, c2_ref)):
        q = new_ref[3 * g] * scale
        k_new = new_ref[3 * g + 1]
        v_new = new_ref[3 * g + 2]
        k_c = c_ref[:, 0]
        v_c = c_ref[:, 1]
        s_c = jnp.sum(k_c * q[None], axis=-1, keepdims=True)
        s_n = jnp.sum(k_new * q, axis=-1, keepdims=True)
        m = jnp.maximum(jnp.max(s_c, axis=0), s_n)
        p_c = jnp.exp(s_c - m[None])
        p_n = jnp.exp(s_n - m)
        den = jnp.sum(p_c, axis=0) + p_n
        o = (jnp.sum(p_c * v_c, axis=0) + p_n * v_new) / den
        outs.append(o)
        lses.append(jnp.broadcast_to(m + jnp.log(den), o.shape))
    o_ref[...] = _merge_groups(outs, lses)


def _dil_sample_attention(new_qkv, caches):
    n = new_qkv.shape[0]
    h, hd = DIL_HEADS, DIL_HEAD_DIM
    cache_specs, cache_views = [], []
    for (window, dil), c in zip(DIL_GROUPS, caches):
        assert c.shape[1] == window, "cache must hold one full window"
        n_keys = window // dil
        cache_views.append(c.reshape(n, n_keys, dil, 2, h, hd))
        cache_specs.append(pl.BlockSpec((None, n_keys, None, 2, h, hd), lambda i: (i, 0, 0, 0, 0, 0)))
    return pl.pallas_call(
        _dil_sample_kernel,
        grid=(n,),
        in_specs=[pl.BlockSpec((None, 9, h, hd), lambda i: (i, 0, 0, 0))] + cache_specs,
        out_specs=pl.BlockSpec((None, h, hd), lambda i: (i, 0, 0)),
        out_shape=jax.ShapeDtypeStruct((n, h, hd), F32),
        compiler_params=_cparams(("parallel",)),
        name="dil_sample_attention",
    )(new_qkv, *cache_views)


def _router_kernel(x_ref, wh_ref, wl_ref, idx_ref, gate_ref):
    x = x_ref[...]
    xh = x.astype(BF16)
    xl = (x - xh.astype(F32)).astype(BF16)
    wh, wl = wh_ref[...], wl_ref[...]
    logits = (jnp.dot(xh, wh, preferred_element_type=F32) + jnp.dot(xl, wh, preferred_element_type=F32)
              + jnp.dot(xh, wl, preferred_element_type=F32))
    lane = lax.broadcasted_iota(jnp.int32, logits.shape, 1)
    neg = -jnp.inf
    lg = jnp.where(lane < N_EXPERTS, logits, neg)
    v1 = jnp.max(lg, axis=-1, keepdims=True)
    i1 = jnp.min(jnp.where(lg == v1, lane, LANES), axis=-1, keepdims=True)
    lg2 = jnp.where(lane == i1, neg, lg)
    v2 = jnp.max(lg2, axis=-1, keepdims=True)
    i2 = jnp.min(jnp.where(lg2 == v2, lane, LANES), axis=-1, keepdims=True)
    e = jnp.exp(v2 - v1)
    g1 = 1.0 / (1.0 + e)
    g2 = e / (1.0 + e)
    idx_ref[...] = jnp.where(lane == 0, i1, jnp.where(lane == 1, i2, 0))
    gate_ref[...] = jnp.where(lane == 0, g1, jnp.where(lane == 1, g2, 0.0))


def _router(x, w_hi, w_lo):
    m, d = x.shape
    tm = _row_tile(m, MM_ROWS)
    return pl.pallas_call(
        _router_kernel,
        grid=(m // tm,),
        in_specs=[pl.BlockSpec((tm, d), lambda i: (i, 0)),
                  pl.BlockSpec((d, LANES), lambda i: (0, 0)),
                  pl.BlockSpec((d, LANES), lambda i: (0, 0))],
        out_specs=[pl.BlockSpec((tm, LANES), lambda i: (i, 0)),
                   pl.BlockSpec((tm, LANES), lambda i: (i, 0))],
        out_shape=[jax.ShapeDtypeStruct((m, LANES), jnp.int32), jax.ShapeDtypeStruct((m, LANES), F32)],
        compiler_params=_cparams(("parallel",)),
        name="moe_router",
    )(x, w_hi, w_lo)


def _row_copy(src_hbm, dst_vmem, sem, src_row, dst_row):
    return pltpu.make_async_copy(src_hbm.at[pl.ds(src_row, 1)], dst_vmem.at[pl.ds(dst_row, 1)], sem)


def _moe_kernel(be_ref, bv_ref, lv_ref, tok_ref, x_hbm, wg_ref, wu_ref, wd_ref, y_ref, xrow_ref, xb_ref, sem):
    b = pl.program_id(0)
    f = pl.program_id(1)
    tm = xrow_ref.shape[0]

    @pl.when(bv_ref[b] == 1)
    def _():
        @pl.when(f == 0)
        def _():
            def start(i, c):
                _row_copy(x_hbm, xrow_ref, sem, tok_ref[0, i], i).start()
                return c

            def wait(i, c):
                _row_copy(x_hbm, xrow_ref, sem, tok_ref[0, i], i).wait()
                return c

            lax.fori_loop(0, tm, start, 0)
            lax.fori_loop(0, tm, wait, 0)
            xb_ref[...] = xrow_ref[...].astype(BF16)
            y_ref[...] = jnp.zeros_like(y_ref)

        x = xb_ref[...]
        gate = jnp.dot(x, wg_ref[...], preferred_element_type=F32)
        up = jnp.dot(x, wu_ref[...], preferred_element_type=F32)
        h = (_silu(gate) * up).astype(BF16)
        y_ref[...] += jnp.dot(h, wd_ref[...], preferred_element_type=F32)

    @pl.when((bv_ref[b] == 0) & (f == 0))
    def _():
        y_ref[...] = jnp.zeros_like(y_ref)


def _moe_experts(x, row_tok, block_expert, block_valid, last_valid, wg, wu, wd, tf):
    d = x.shape[1]
    d_ff = wg.shape[2]
    n_blocks, _, tm = row_tok.shape
    nf = d_ff // tf

    def f_idx(b, f, bv):
        return jnp.where(bv[b] == 1, f, nf - 1)

    grid_spec = pltpu.PrefetchScalarGridSpec(
        num_scalar_prefetch=3,
        grid=(n_blocks, nf),
        in_specs=[pl.BlockSpec((None, 1, tm), lambda b, f, be, bv, lv: (jnp.minimum(b, lv[0]), 0, 0),
                               memory_space=pltpu.SMEM),
                  pl.BlockSpec(memory_space=pl.ANY),
                  pl.BlockSpec((None, d, tf), lambda b, f, be, bv, lv: (be[b], 0, f_idx(b, f, bv))),
                  pl.BlockSpec((None, d, tf), lambda b, f, be, bv, lv: (be[b], 0, f_idx(b, f, bv))),
                  pl.BlockSpec((None, tf, d), lambda b, f, be, bv, lv: (be[b], f_idx(b, f, bv), 0))],
        out_specs=pl.BlockSpec((tm, d), lambda b, f, be, bv, lv: (b, 0)),
        scratch_shapes=[pltpu.VMEM((tm, d), F32), pltpu.VMEM((tm, d), BF16), pltpu.SemaphoreType.DMA(())],
    )
    return pl.pallas_call(
        _moe_kernel,
        grid_spec=grid_spec,
        out_shape=jax.ShapeDtypeStruct((n_blocks * tm, d), F32),
        compiler_params=_cparams(("arbitrary", "arbitrary")),
        name="moe_experts",
    )(block_expert, block_valid, last_valid, row_tok, x, wg, wu, wd)


def _combine_kernel(pos_ref, y_hbm, res_ref, gate_ref, g_ref, b_ref, o_ref, y1_ref, y2_ref, sem):
    tm = res_ref.shape[0]

    def start(i, c):
        _row_copy(y_hbm, y1_ref, sem.at[0], pos_ref[0, i], i).start()
        _row_copy(y_hbm, y2_ref, sem.at[1], pos_ref[1, i], i).start()
        return c

    def wait(i, c):
        _row_copy(y_hbm, y1_ref, sem.at[0], pos_ref[0, i], i).wait()
        _row_copy(y_hbm, y2_ref, sem.at[1], pos_ref[1, i], i).wait()
        return c

    lax.fori_loop(0, tm, start, 0)
    lax.fori_loop(0, tm, wait, 0)
    gates = gate_ref[...]
    y = y1_ref[...] * gates[:, 0:1] + y2_ref[...] * gates[:, 1:2]
    o_ref[...] = _layer_norm(DEEPNORM_ALPHA * res_ref[...] + y, g_ref[...], b_ref[...])


def _moe_combine_ln(pos, y_rows, res, gates, g, b):
    m, d = res.shape
    tm = pos.shape[2]
    return pl.pallas_call(
        _combine_kernel,
        grid=(m // tm,),
        in_specs=[pl.BlockSpec((None, 2, tm), lambda i: (i, 0, 0), memory_space=pltpu.SMEM),
                  pl.BlockSpec(memory_space=pl.ANY),
                  pl.BlockSpec((tm, d), lambda i: (i, 0)),
                  pl.BlockSpec((tm, LANES), lambda i: (i, 0)),
                  pl.BlockSpec((1, d), lambda i: (0, 0)),
                  pl.BlockSpec((1, d), lambda i: (0, 0))],
        out_specs=pl.BlockSpec((tm, d), lambda i: (i, 0)),
        out_shape=jax.ShapeDtypeStruct((m, d), F32),
        scratch_shapes=[pltpu.VMEM((tm, d), F32), pltpu.VMEM((tm, d), F32), pltpu.SemaphoreType.DMA((2,))],
        compiler_params=_cparams(("arbitrary",)),
        name="moe_combine_ln",
    )(pos, y_rows, res, gates, g, b)


def _route_rows(idx, tm, tm_tok):
    n_tok = idx.shape[0]
    n_pairs = n_tok * TOP_K
    experts = jnp.arange(N_EXPERTS, dtype=jnp.int32)
    flat_e = idx[:, :TOP_K].reshape(-1)
    onehot = (flat_e[:, None] == experts[None, :]).astype(jnp.int32)
    csum = jnp.cumsum(onehot, axis=0)
    counts = csum[-1]
    rank = jnp.sum((csum - onehot) * onehot, axis=1)
    padded = (counts + tm - 1) // tm * tm
    start = jnp.cumsum(counts) - counts
    pend = jnp.cumsum(padded)
    pstart = pend - padded
    dest = jnp.sum(onehot * pstart[None, :], axis=1) + rank
    pos = dest.reshape(n_tok // tm_tok, tm_tok, TOP_K).transpose(0, 2, 1)

    n_blocks = -(-n_pairs // tm) + N_EXPERTS
    order = jnp.argsort(flat_e, stable=True).astype(jnp.int32)
    r = jnp.arange(n_blocks * tm, dtype=jnp.int32)
    e_r = jnp.minimum(jnp.sum((r[:, None] >= pend[None, :]).astype(jnp.int32), axis=1), N_EXPERTS - 1)
    sel = (e_r[:, None] == experts[None, :]).astype(jnp.int32)
    off = r - jnp.sum(sel * pstart[None, :], axis=1)
    valid = off < jnp.sum(sel * counts[None, :], axis=1)
    src = jnp.clip(jnp.sum(sel * start[None, :], axis=1) + off, 0, n_pairs - 1)
    row_tok = jnp.where(valid, order[src] // TOP_K, 0).astype(jnp.int32)

    blk_start = r[::tm]
    block_valid = (blk_start < pend[-1]).astype(jnp.int32)
    last_valid = (pend[-1] // tm - 1).astype(jnp.int32).reshape(1)
    block_expert = e_r[::tm]
    block_expert = jnp.where(block_valid == 1, block_expert, block_expert[last_valid[0]]).astype(jnp.int32)
    return row_tok.reshape(n_blocks, 1, tm), pos.astype(jnp.int32), block_expert, block_valid, last_valid


def kernel(x_prompt, x_sample, state_gla, cache_win128, cache_win512, cache_win2048, gla_w_in, gla_w_gate_up, gla_b_gate, gla_norm_g, gla_w_out, ln_mix0_g, ln_mix0_b, ffn_w_gate, ffn_w_up, ffn_w_down, ln_ffn0_g, ln_ffn0_b, dil_w_in, dil_w_out, ln_mix1_g, ln_mix1_b, moe_w_router, moe_w_gate, moe_w_up, moe_w_down, ln_ffn1_g, ln_ffn1_b):
    n_p, seq_len, d = x_prompt.shape
    n_s, dec_len, _ = x_sample.shape
    assert dec_len == 1 and seq_len % 2048 == 0
    rows_p, rows_s = n_p * seq_len, n_s * dec_len
    n_tok = rows_p + rows_s
    row = lambda a: a.reshape(1, -1)

    x0 = jnp.concatenate([x_prompt.reshape(rows_p, d), x_sample.reshape(rows_s, d)], axis=0)
    x0b = x0.astype(BF16)

    n_main = gla_w_in.shape[1] - GLA_GATE_RANK
    proj = _matmul(x0b, gla_w_in[:, :n_main].astype(BF16), 1024)
    w_low = jnp.pad(gla_w_in[:, n_main:], ((0, 0), (0, LANES - GLA_GATE_RANK))).astype(BF16)
    w_up = jnp.pad(gla_w_gate_up, ((0, LANES - GLA_GATE_RANK), (0, 0))).astype(BF16)
    log_a = _gla_gate(x0b, w_low, w_up, row(gla_b_gate))

    s_zero = jnp.zeros((n_p,) + state_gla.shape[1:], F32)
    og_p, state_p = _gla_recurrence(proj, log_a, s_zero, row(gla_norm_g), n_p, seq_len, GLA_T_BLOCK)
    pad_t = lambda a: jnp.pad(a[rows_p:].reshape(n_s, 1, -1),
                              ((0, 0), (0, GLA_CHUNK - 1), (0, 0))).reshape(n_s * GLA_CHUNK, -1)
    og_s, state_s = _gla_recurrence(pad_t(proj), pad_t(log_a), state_gla, row(gla_norm_g), n_s, GLA_CHUNK, GLA_CHUNK)
    og = jnp.concatenate([og_p, og_s[::GLA_CHUNK]], axis=0)
    x1 = _proj_ln(og, gla_w_out.astype(BF16), x0, row(ln_mix0_g), row(ln_mix0_b), "gla_out_ln")

    x2, x2b = _ffn_ln(x1, ffn_w_gate.astype(BF16), ffn_w_up.astype(BF16), ffn_w_down.astype(BF16),
                      row(ln_ffn0_g), row(ln_ffn0_b), 512)

    pos = jnp.concatenate([jnp.tile(jnp.arange(seq_len, dtype=jnp.int32), n_p),
                           jnp.tile(PAST_LEN + jnp.arange(dec_len, dtype=jnp.int32), n_s)])
    qkv = _qkv_rope(x2b, dil_w_in.astype(BF16), *_rope_tables(pos))
    att_p = _dil_prompt_attention(qkv, n_p, seq_len)
    n_g = len(DIL_GROUPS)
    new_qkv = qkv[rows_p:].reshape(n_s, 3 * n_g, DIL_HEADS, DIL_HEAD_DIM)
    caches = (cache_win128, cache_win512, cache_win2048)
    att_s = _dil_sample_attention(new_qkv, caches)
    att = jnp.concatenate([att_p, att_s.reshape(rows_s, -1).astype(BF16)], axis=0)
    x3 = _proj_ln(att, dil_w_out.astype(BF16), x2, row(ln_mix1_g), row(ln_mix1_b), "dil_out_ln")

    wins_p, wins_s = [], []
    for g, (window, _) in enumerate(DIL_GROUPS):
        width = 2 * DIL_HEADS * DIL_HEAD_DIM
        kv = qkv[:, (3 * g + 1) * width // 2:(3 * g + 3) * width // 2]
        kv_p = kv[:rows_p].reshape(n_p, seq_len, 2, DIL_HEADS, DIL_HEAD_DIM)
        wins_p.append(kv_p[:, seq_len - min(window, seq_len):])
        kv_s = kv[rows_p:].reshape(n_s, dec_len, 2, DIL_HEADS, DIL_HEAD_DIM)
        wins_s.append(jnp.concatenate([caches[g], kv_s], axis=1)[:, dec_len:])

    w_r = jnp.pad(moe_w_router, ((0, 0), (0, LANES - N_EXPERTS)))
    w_r_hi = w_r.astype(BF16)
    w_r_lo = (w_r - w_r_hi.astype(F32)).astype(BF16)
    idx, gates = _router(x3, w_r_hi, w_r_lo)
    tm_tok = _row_tile(n_tok, FFN_ROWS)
    row_tok, pos_rows, block_expert, block_valid, last_valid = _route_rows(idx, MOE_ROWS, tm_tok)
    y_rows = _moe_experts(x3, row_tok, block_expert, block_valid, last_valid,
                          moe_w_gate.astype(BF16), moe_w_up.astype(BF16), moe_w_down.astype(BF16), 1024)
    x4 = _moe_combine_ln(pos_rows, y_rows, x3, gates, row(ln_ffn1_g), row(ln_ffn1_b))

    y_p = x4[:rows_p].reshape(n_p, seq_len, d)
    y_s = x4[rows_p:].reshape(n_s, dec_len, d)
    return (y_p, y_s, state_p, wins_p[0], wins_p[1], wins_p[2], state_s, wins_s[0], wins_s[1], wins_s[2])
```
